```python
import jax
import jax.numpy as jnp
from jax import lax
import numpy as np


D_MODEL = 2048
BATCH = 4
SEQ = 2048
DEPTH = 1

GRID_W = 64
NA_HEADS = 16
NA_HEAD_DIM = 64
NA_WIDTH = NA_HEADS * NA_HEAD_DIM
NA_WIN_R = 8
NA_WIN_C = 16
GLA_HEADS = 4
GLA_DK = D_MODEL // (2 * GLA_HEADS)
GLA_DV = D_MODEL // GLA_HEADS
GLA_K = GLA_HEADS * GLA_DK
GLA_V = GLA_HEADS * GLA_DV
GLA_GATE_RANK = 16
GLA_TAU = 16.0
GLA_CHUNK = 64
PEER_HEADS = 8
PEER_NKEYS = 128
PEER_N = PEER_NKEYS * PEER_NKEYS
PEER_DKEY = 256
PEER_TOPK = 16
PEER_BLOCK = 128
PLE_DIM = 256
LN_EPS = 1e-5
DEEPNORM_ALPHA = (2.0 * DEPTH) ** 0.25
DEEPNORM_BETA = (8.0 * DEPTH) ** -0.25

IN_SPLITS = (NA_WIDTH, NA_WIDTH, NA_WIDTH, GLA_K, GLA_K, GLA_V, GLA_V, GLA_GATE_RANK, GLA_GATE_RANK, D_MODEL, D_MODEL)
IN_WIDTH = sum(IN_SPLITS)
IN_SPLIT_POINTS = tuple(int(c) for c in np.cumsum(IN_SPLITS)[:-1])

kernel_name = 'hybrid_natten_gla_peer_encoder'


def layer_norm(x, g, b):
    xf = x.astype(jnp.float32)
    mu = jnp.mean(xf, axis=-1, keepdims=True)
    var = jnp.mean(jnp.square(xf - mu), axis=-1, keepdims=True)
    y = (xf - mu) * lax.rsqrt(var + LN_EPS)
    return (y * g.astype(jnp.float32) + b.astype(jnp.float32)).astype(x.dtype)


def to_heads(t, n_heads):
    bsz, s, _ = t.shape
    return t.reshape(bsz, s, n_heads, -1).transpose(0, 2, 1, 3)


def neighbourhood_attention(q, k, v, rpb):
    bsz, s, h, dh = q.shape
    rows = s // GRID_W
    kr = min(NA_WIN_R, rows)
    kc = NA_WIN_C

    def to_grid(t):
        return t.reshape(bsz, rows, GRID_W, h, dh).transpose(1, 0, 3, 2, 4)

    qg = to_grid(q * (dh ** -0.5))
    kg = to_grid(k)
    vg = to_grid(v)
    cols = jnp.arange(GRID_W)
    col_idx = jnp.clip(cols - kc // 2, 0, GRID_W - kc)[:, None] + jnp.arange(kc)[None, :]
    dc = col_idx - cols[:, None] + (NA_WIN_C - 1)

    def row_block(r):
        r0 = jnp.clip(r - kr // 2, 0, rows - kr)
        k_win = lax.dynamic_slice_in_dim(kg, r0, kr, axis=0)[:, :, :, col_idx, :]
        v_win = lax.dynamic_slice_in_dim(vg, r0, kr, axis=0)[:, :, :, col_idx, :]
        dr = r0 + jnp.arange(kr) - r + (NA_WIN_R - 1)
        bias = rpb[:, dr[:, None, None], dc[None, :, :]].transpose(0, 2, 1, 3)
        scores = jnp.einsum('bhwd,ibhwjd->bhwij', qg[r], k_win).astype(jnp.float32)
        scores = scores + bias.astype(jnp.float32)[None]
        probs = jax.nn.softmax(scores.reshape(bsz, h, GRID_W, kr * kc), axis=-1)
        probs = probs.reshape(bsz, h, GRID_W, kr, kc).astype(v.dtype)
        return jnp.einsum('bhwij,ibhwjd->bhwd', probs, v_win)

    out = lax.map(row_block, jnp.arange(rows))
    return out.transpose(1, 0, 3, 2, 4).reshape(bsz, s, h * dh)


def gla_chunked(q, k, v, log_a, include_diag):
    bsz, h, s, dk = q.shape
    dv = v.shape[-1]
    c = GLA_CHUNK
    n = s // c
    f32 = jnp.float32
    qc = q.astype(f32).reshape(bsz, h, n, c, dk)
    kc = k.astype(f32).reshape(bsz, h, n, c, dk)
    vc = v.astype(f32).reshape(bsz, h, n, c, dv)
    b = jnp.cumsum(log_a.astype(f32).reshape(bsz, h, n, c, dk), axis=3)
    b_last = b[:, :, :, -1:, :]
    q_dec = qc * jnp.exp(b)
    k_intra = kc * jnp.exp(-b)
    k_state = kc * jnp.exp(b_last - b)
    mask = jnp.tril(jnp.ones((c, c), dtype=bool), 0 if include_diag else -1)
    attn = jnp.where(mask, jnp.einsum('bhnik,bhnjk->bhnij', q_dec, k_intra), 0.0)
    o_intra = jnp.einsum('bhnij,bhnjv->bhniv', attn, vc)

    def step(state, xs):
        q_d, k_s, v_n, decay = xs
        o = jnp.einsum('bhck,bhkv->bhcv', q_d, state)
        state = decay[..., None] * state + jnp.einsum('bhck,bhcv->bhkv', k_s, v_n)
        return state, o

    xs = (jnp.moveaxis(q_dec, 2, 0), jnp.moveaxis(k_state, 2, 0), jnp.moveaxis(vc, 2, 0),
          jnp.moveaxis(jnp.exp(b_last[:, :, :, 0, :]), 2, 0))
    _, o_inter = lax.scan(step, jnp.zeros((bsz, h, dk, dv), f32), xs)
    o = o_intra + jnp.moveaxis(o_inter, 0, 2)
    return o.reshape(bsz, h, s, dv)


def peer(x, w_q, sub_keys, u, v):
    bsz, s, d = x.shape
    kk = PEER_TOPK
    q = (x @ w_q).reshape(bsz, s, PEER_HEADS, 2, PEER_DKEY // 2)
    sc = jnp.einsum('bshpk,hpnk->bshpn', q, sub_keys).astype(jnp.float32)
    val, idx = lax.top_k(sc, kk)
    cand = val[..., 0, :, None] + val[..., 1, None, :]
    cand_idx = idx[..., 0, :, None] * PEER_NKEYS + idx[..., 1, None, :]
    top_val, top_pos = lax.top_k(cand.reshape(bsz, s, PEER_HEADS, kk * kk), kk)
    expert = jnp.take_along_axis(cand_idx.reshape(bsz, s, PEER_HEADS, kk * kk), top_pos, axis=-1)
    gate = jax.nn.softmax(top_val, axis=-1)
    n_blocks = (bsz * s) // PEER_BLOCK
    xb = x.reshape(n_blocks, PEER_BLOCK, d)
    eb = expert.reshape(n_blocks, PEER_BLOCK, PEER_HEADS * kk)
    gb = gate.reshape(n_blocks, PEER_BLOCK, PEER_HEADS * kk).astype(x.dtype)

    def block(args):
        xt, et, gt = args
        hid = jnp.einsum('td,ted->te', xt, u[et])
        w = gt * jax.nn.gelu(hid, approximate=False)
        return jnp.einsum('te,ted->td', w, v[et])

    y = lax.map(block, (xb, eb, gb))
    return y.reshape(bsz, s, d)


def setup_inputs(seed: int = 0) -> dict:
    key = jax.random.key(seed)
    ks = jax.random.split(key, 25)

    def nrm(k, shape, scale):
        return jax.random.normal(k, shape, jnp.float32) * scale

    L, D = DEPTH, D_MODEL
    return {
        'x': nrm(ks[0], (BATCH, SEQ, D), 1.0),
        'p': nrm(ks[1], (L, BATCH, SEQ, PLE_DIM), 1.0),
        'ln_emb_g': 1.0 + nrm(ks[2], (D,), 0.02),
        'ln_emb_b': nrm(ks[3], (D,), 0.02),
        'w_in': nrm(ks[4], (L, D, IN_WIDTH), D ** -0.5),
        'b_in': nrm(ks[5], (L, IN_WIDTH), 0.01),
        'na_rpb': nrm(ks[6], (L, NA_HEADS, 2 * NA_WIN_R - 1, 2 * NA_WIN_C - 1), 0.1),
        'gla_gk_up_f': nrm(ks[7], (L, GLA_GATE_RANK, GLA_K), GLA_GATE_RANK ** -0.5),
        'gla_gk_bias_f': nrm(ks[8], (L, GLA_K), 0.01),
        'gla_gk_up_b': nrm(ks[9], (L, GLA_GATE_RANK, GLA_K), GLA_GATE_RANK ** -0.5),
        'gla_gk_bias_b': nrm(ks[10], (L, GLA_K), 0.01),
        'gla_norm_g': 1.0 + nrm(ks[11], (L, GLA_DV), 0.02),
        'w_proj_a': nrm(ks[12], (L, NA_WIDTH, D), NA_WIDTH ** -0.5),
        'w_proj_b': nrm(ks[13], (L, GLA_V, D), GLA_V ** -0.5),
        'w_out': nrm(ks[14], (L, D, D), DEEPNORM_BETA * D ** -0.5),
        'ln_a_g': 1.0 + nrm(ks[15], (L, D), 0.02),
        'ln_a_b': nrm(ks[16], (L, D), 0.02),
        'peer_wq': nrm(ks[17], (L, D, PEER_HEADS * PEER_DKEY), D ** -0.5),
        'peer_sub_keys': nrm(ks[18], (L, PEER_HEADS, 2, PEER_NKEYS, PEER_DKEY // 2), (PEER_DKEY // 2) ** -0.5),
        'peer_u': nrm(ks[19], (L, PEER_N, D), D ** -0.5),
        'peer_v': nrm(ks[20], (L, PEER_N, D), DEEPNORM_BETA),
        'w_ple': nrm(ks[21], (L, PLE_DIM, D), DEEPNORM_BETA * PLE_DIM ** -0.5),
        'w_ple_gate': nrm(ks[22], (L, D, D), D ** -0.5),
        'ln_b_g': 1.0 + nrm(ks[23], (L, D), 0.02),
        'ln_b_b': nrm(ks[24], (L, D), 0.02),
    }


def reference(x, p, ln_emb_g, ln_emb_b, w_in, b_in, na_rpb, gla_gk_up_f, gla_gk_bias_f,
              gla_gk_up_b, gla_gk_bias_b, gla_norm_g, w_proj_a, w_proj_b, w_out, ln_a_g, ln_a_b,
              peer_wq, peer_sub_keys, peer_u, peer_v, w_ple, w_ple_gate, ln_b_g, ln_b_b):
    bsz, s, d = x.shape
    f32 = jnp.float32
    h = layer_norm(x, ln_emb_g, ln_emb_b)
    for i in range(DEPTH):
        z = h @ w_in[i] + b_in[i]
        (na_q, na_k, na_v, g_q, g_k, g_v, g_out, gk_f_lr, gk_b_lr, gate_a, gate_b) = jnp.split(
            z, IN_SPLIT_POINTS, axis=-1)
        hd = (bsz, s, NA_HEADS, NA_HEAD_DIM)
        y_na = neighbourhood_attention(na_q.reshape(hd), na_k.reshape(hd), na_v.reshape(hd), na_rpb[i])
        y_a = y_na @ w_proj_a[i]
        gq = to_heads(g_q, GLA_HEADS) * (GLA_DK ** -0.5)
        gk = to_heads(g_k, GLA_HEADS)
        gv = to_heads(g_v, GLA_HEADS)
        la_f = to_heads(jax.nn.log_sigmoid((gk_f_lr @ gla_gk_up_f[i] + gla_gk_bias_f[i]).astype(f32)) / GLA_TAU, GLA_HEADS)
        la_b = to_heads(jax.nn.log_sigmoid((gk_b_lr @ gla_gk_up_b[i] + gla_gk_bias_b[i]).astype(f32)) / GLA_TAU, GLA_HEADS)
        o_f = gla_chunked(gq, gk, gv, la_f, True)
        o_b = jnp.flip(gla_chunked(jnp.flip(gq, 2), jnp.flip(gk, 2), jnp.flip(gv, 2), jnp.flip(la_b, 2), False), 2)
        o = (o_f + o_b).transpose(0, 2, 1, 3)
        o = o * lax.rsqrt(jnp.mean(o * o, axis=-1, keepdims=True) + LN_EPS) * gla_norm_g[i].astype(f32)
        o = o.astype(x.dtype) * jax.nn.silu(g_out.reshape(bsz, s, GLA_HEADS, GLA_DV))
        y_b = o.reshape(bsz, s, GLA_V) @ w_proj_b[i]
        merged = jax.nn.sigmoid(gate_a) * y_a + jax.nn.sigmoid(gate_b) * y_b
        h = layer_norm(DEEPNORM_ALPHA * h + merged @ w_out[i], ln_a_g[i], ln_a_b[i])
        ple = (p[i] @ w_ple[i]) * jax.nn.sigmoid(h @ w_ple_gate[i])
        y_c = peer(h, peer_wq[i], peer_sub_keys[i], peer_u[i], peer_v[i])
        h = layer_norm(DEEPNORM_ALPHA * h + y_c + ple, ln_b_g[i], ln_b_b[i])
    return h
```

```python
import functools
import math

import jax
import jax.numpy as jnp
from jax import lax
from jax.experimental import pallas as pl
from jax.experimental.pallas import tpu as pltpu

f32 = jnp.float32
bf16 = jnp.bfloat16

D_MODEL = 2048
GRID_W = 64
NA_HEADS = 16
NA_HEAD_DIM = 64
NA_WIDTH = NA_HEADS * NA_HEAD_DIM
NA_WIN_R = 8
NA_WIN_C = 16
GLA_HEADS = 4
GLA_DK = D_MODEL // (2 * GLA_HEADS)
GLA_DV = D_MODEL // GLA_HEADS
GLA_K = GLA_HEADS * GLA_DK
GLA_V = GLA_HEADS * GLA_DV
GLA_GATE_RANK = 16
GLA_TAU = 16.0
GLA_CHUNK = 64
PEER_HEADS = 8
PEER_NKEYS = 128
PEER_N = PEER_NKEYS * PEER_NKEYS
PEER_DKEY = 256
PEER_TOPK = 16
PLE_DIM = 256
LN_EPS = 1e-5
DEPTH = 1
DEEPNORM_ALPHA = (2.0 * DEPTH) ** 0.25

_OFF_NA_Q = 0
_OFF_GLA_Q = 3 * NA_WIDTH
_OFF_GLA_K = _OFF_GLA_Q + GLA_K
_OFF_GLA_V = _OFF_GLA_K + GLA_K
_OFF_GLA_G = _OFF_GLA_V + GLA_V
_OFF_LR = _OFF_GLA_G + GLA_V
_OFF_GATE = _OFF_LR + 2 * GLA_GATE_RANK
Z_MAIN = _OFF_LR

LANES = 128
VMEM_LIMIT = 56 * 1024 * 1024

_NT = (((1,), (1,)), ((), ()))
_TN = (((0,), (0,)), ((), ()))


def _params(sem, vmem=VMEM_LIMIT):
    return pltpu.CompilerParams(dimension_semantics=sem, vmem_limit_bytes=vmem)


def _layer_norm(x, g, b):
    mu = jnp.mean(x, axis=-1, keepdims=True)
    var = jnp.mean(jnp.square(x - mu), axis=-1, keepdims=True)
    return (x - mu) * lax.rsqrt(var + LN_EPS) * g + b


def _dot(a, b):
    return jnp.dot(a, b, preferred_element_type=f32)


def _ln_inproj_kernel(x_ref, g_ref, b_ref, w_ref, bias_ref, wlr_ref, blr_ref, z_ref, h0b_ref, lr_ref):
    @pl.when(pl.program_id(1) == 0)
    def _():
        hb = _layer_norm(x_ref[...], g_ref[...], b_ref[...]).astype(bf16)
        h0b_ref[...] = hb
        lr_ref[...] = _dot(hb, wlr_ref[...].astype(bf16)) + blr_ref[...]

    z = _dot(h0b_ref[...], w_ref[...].astype(bf16)) + bias_ref[...]
    z_ref[...] = z.astype(bf16)


def _ln_inproj(x, g, b, w_in, b_in, w_lr, b_lr, tm, tn):
    n, d = x.shape
    r = w_lr.shape[1]
    return pl.pallas_call(
        _ln_inproj_kernel,
        grid=(n // tm, Z_MAIN // tn),
        in_specs=[
            pl.BlockSpec((tm, d), lambda i, j: (i, 0)),
            pl.BlockSpec((1, d), lambda i, j: (0, 0)),
            pl.BlockSpec((1, d), lambda i, j: (0, 0)),
            pl.BlockSpec((d, tn), lambda i, j: (0, j)),
            pl.BlockSpec((1, tn), lambda i, j: (0, j)),
            pl.BlockSpec((d, r), lambda i, j: (0, 0)),
            pl.BlockSpec((1, r), lambda i, j: (0, 0)),
        ],
        out_specs=[
            pl.BlockSpec((tm, tn), lambda i, j: (i, j)),
            pl.BlockSpec((tm, d), lambda i, j: (i, 0)),
            pl.BlockSpec((tm, r), lambda i, j: (i, 0)),
        ],
        out_shape=[
            jax.ShapeDtypeStruct((n, Z_MAIN), bf16),
            jax.ShapeDtypeStruct((n, d), bf16),
            jax.ShapeDtypeStruct((n, r), f32),
        ],
        compiler_params=_params(("parallel", "arbitrary")),
        name="ln_inproj",
    )(x, g, b, w_in, b_in, w_lr, b_lr)


def _na_bias_table(rpb, rows):
    w = GRID_W
    kr = min(NA_WIN_R, rows)
    kc = NA_WIN_C
    cols = jnp.arange(w)
    c0 = jnp.clip(cols - kc // 2, 0, w - kc)
    valid = (cols[None, :] >= c0[:, None]) & (cols[None, :] < c0[:, None] + kc)
    dc = jnp.clip(cols[None, :] - cols[:, None] + (NA_WIN_C - 1), 0, 2 * NA_WIN_C - 2)
    dr = jnp.arange(kr)[None, :] - jnp.arange(kr)[:, None] + (NA_WIN_R - 1)
    tab = rpb[:, dr[:, :, None, None], dc[None, None, :, :]]
    tab = jnp.where(valid[None, None, None], tab, -1e30)
    return tab.transpose(1, 0, 3, 2, 4).reshape(kr, rpb.shape[0], w, kr * w).astype(f32)


def _natten_kernel(q_ref, k_ref, v_ref, bias_ref, o_ref, *, rows, kr):
    r = pl.program_id(1)
    r0 = jnp.clip(r - kr // 2, 0, rows - kr)
    start = pl.multiple_of(r0 * GRID_W, GRID_W)
    win = kr * GRID_W
    lane = lax.broadcasted_iota(jnp.int32, (GRID_W, LANES), 1)
    lo = lane < NA_HEAD_DIM
    heads_per_group = LANES // NA_HEAD_DIM
    for grp in range(NA_HEADS // heads_per_group):
        cs = slice(grp * LANES, (grp + 1) * LANES)
        qp = q_ref[:, cs]
        kp = k_ref[pl.ds(start, win), cs]
        vp = v_ref[pl.ds(start, win), cs]
        outs = []
        for half in range(heads_per_group):
            sel = lo if half == 0 else jnp.logical_not(lo)
            qm = jnp.where(sel, qp, jnp.zeros_like(qp))
            s = lax.dot_general(qm, kp, _NT, preferred_element_type=f32) * (NA_HEAD_DIM ** -0.5)
            s = s + bias_ref[0, grp * heads_per_group + half]
            p = jnp.exp(s - jnp.max(s, axis=-1, keepdims=True))
            l = jnp.sum(p, axis=-1, keepdims=True)
            outs.append(_dot(p.astype(bf16), vp) / l)
        o_ref[:, cs] = jnp.where(lo, outs[0], outs[1]).astype(bf16)


def _natten(z, bias_tab, bsz, seq):
    n = z.shape[0]
    rows = seq // GRID_W
    kr = min(NA_WIN_R, rows)

    def d_of(r):
        return r - jnp.clip(r - kr // 2, 0, rows - kr)

    return pl.pallas_call(
        functools.partial(_natten_kernel, rows=rows, kr=kr),
        grid=(bsz, rows),
        in_specs=[
            pl.BlockSpec((GRID_W, NA_WIDTH), lambda b, r: (b * rows + r, 0)),
            pl.BlockSpec((seq, NA_WIDTH), lambda b, r: (b, 1)),
            pl.BlockSpec((seq, NA_WIDTH), lambda b, r: (b, 2)),
            pl.BlockSpec((1, NA_HEADS, GRID_W, kr * GRID_W), lambda b, r: (d_of(r), 0, 0, 0)),
        ],
        out_specs=pl.BlockSpec((GRID_W, NA_WIDTH), lambda b, r: (b * rows + r, 0)),
        out_shape=jax.ShapeDtypeStruct((n, NA_WIDTH), bf16),
        compiler_params=_params(("parallel", "arbitrary")),
        name="natten",
    )(z, z, z, bias_tab)


def _gla_kernel(*refs, reverse):
    if reverse:
        (q_ref, k_ref, v0_ref, v1_ref, lr_ref, up_ref, gb_ref, prev_ref, g0_ref, g1_ref, gn_ref,
         o_ref, state_ref) = refs
    else:
        q_ref, k_ref, v0_ref, v1_ref, lr_ref, up_ref, gb_ref, o_ref, state_ref = refs

    @pl.when(pl.program_id(1) == 0)
    def _():
        state_ref[...] = jnp.zeros_like(state_ref)

    c = GLA_CHUNK
    row = lax.broadcasted_iota(jnp.int32, (c, c), 0)
    col = lax.broadcasted_iota(jnp.int32, (c, c), 1)
    if reverse:
        tri = (col >= row).astype(f32)
        mask = col > row
        last = 0
        lr = lr_ref[:, GLA_GATE_RANK:2 * GLA_GATE_RANK]
    else:
        tri = (col <= row).astype(f32)
        mask = col <= row
        last = c - 1
        lr = lr_ref[:, 0:GLA_GATE_RANK]
    lr = lr.astype(bf16)
    v_refs = (v0_ref, v1_ref)
    heads_per_ref = GLA_HEADS // 2
    for h in range(GLA_HEADS):
        ks = slice(h * GLA_DK, (h + 1) * GLA_DK)
        vs = slice((h % heads_per_ref) * GLA_DV, (h % heads_per_ref + 1) * GLA_DV)
        la = jax.nn.log_sigmoid(_dot(lr, up_ref[:, ks].astype(bf16)) + gb_ref[:, ks]) * (1.0 / GLA_TAU)
        b = jnp.dot(tri, la, preferred_element_type=f32, precision=lax.Precision.HIGHEST)
        b_last = b[last:last + 1, :]
        q = q_ref[:, ks].astype(f32) * (GLA_DK ** -0.5)
        k = k_ref[:, ks].astype(f32)
        v = v_refs[h // heads_per_ref][:, vs]
        qd = (q * jnp.exp(b)).astype(bf16)
        ki = (k * jnp.exp(-b)).astype(bf16)
        kst = (k * jnp.exp(b_last - b)).astype(bf16)
        attn = lax.dot_general(qd, ki, _NT, preferred_element_type=f32)
        attn = jnp.where(mask, attn, 0.0)
        st = state_ref[h]
        o = _dot(attn.astype(bf16), v) + lax.dot_general(qd, st.astype(bf16), _NT, preferred_element_type=f32)
        state_ref[h] = st * jnp.exp(b_last) + lax.dot_general(v, kst, _TN, preferred_element_type=f32)
        os_ = slice(h * GLA_DV, (h + 1) * GLA_DV)
        if reverse:
            o = o + prev_ref[:, os_]
            o = o * lax.rsqrt(jnp.mean(o * o, axis=-1, keepdims=True) + LN_EPS) * gn_ref[...]
            gate = (g0_ref, g1_ref)[h // heads_per_ref][:, vs].astype(f32)
            o_ref[:, os_] = (o * (gate * jax.nn.sigmoid(gate))).astype(bf16)
        else:
            o_ref[:, os_] = o


def _gla(z, lr, up, gbias, bsz, seq, reverse, prev=None, gnorm=None):
    n = z.shape[0]
    c = GLA_CHUNK
    nc = seq // c
    half_v = GLA_V // 2

    if reverse:
        def blk(b, i):
            return b * nc + (nc - 1 - i)
    else:
        def blk(b, i):
            return b * nc + i

    def zspec(width, off):
        assert off % width == 0
        return pl.BlockSpec((c, width), lambda b, i: (blk(b, i), off // width))

    in_specs = [
        zspec(GLA_K, _OFF_GLA_Q), zspec(GLA_K, _OFF_GLA_K),
        zspec(half_v, _OFF_GLA_V), zspec(half_v, _OFF_GLA_V + half_v),
        pl.BlockSpec((c, 2 * GLA_GATE_RANK), lambda b, i: (blk(b, i), 0)),
        pl.BlockSpec((GLA_GATE_RANK, GLA_K), lambda b, i: (0, 0)),
        pl.BlockSpec((1, GLA_K), lambda b, i: (0, 0)),
    ]
    args = [z, z, z, z, lr, up, gbias]
    if reverse:
        in_specs += [
            pl.BlockSpec((c, GLA_V), lambda b, i: (blk(b, i), 0)),
            zspec(half_v, _OFF_GLA_G), zspec(half_v, _OFF_GLA_G + half_v),
            pl.BlockSpec((1, GLA_DV), lambda b, i: (0, 0)),
        ]
        args += [prev, z, z, gnorm]
    return pl.pallas_call(
        functools.partial(_gla_kernel, reverse=reverse),
        grid=(bsz, nc),
        in_specs=in_specs,
        out_specs=pl.BlockSpec((c, GLA_V), lambda b, i: (blk(b, i), 0)),
        out_shape=jax.ShapeDtypeStruct((n, GLA_V), bf16 if reverse else f32),
        scratch_shapes=[pltpu.VMEM((GLA_HEADS, GLA_DV, GLA_DK), f32)],
        compiler_params=_params(("parallel", "arbitrary")),
        name="gla_bwd" if reverse else "gla_fwd",
    )(*args)


def _merge_kernel(h_ref, ya_ref, yb_ref, wga_ref, bga_ref, wgb_ref, bgb_ref, wpa_ref, wpb_ref, o_ref):
    h = h_ref[...]
    ga = jax.nn.sigmoid(_dot(h, wga_ref[...].astype(bf16)) + bga_ref[...])
    gb = jax.nn.sigmoid(_dot(h, wgb_ref[...].astype(bf16)) + bgb_ref[...])
    ya = _dot(ya_ref[...], wpa_ref[...])
    yb = _dot(yb_ref[...], wpb_ref[...])
    o_ref[...] = (ga * ya + gb * yb).astype(bf16)


def _merge(h0b, y_na, og, w_gate, b_gate, w_pa, w_pb, tm, tn):
    n, d = h0b.shape
    nj = d // tn
    return pl.pallas_call(
        _merge_kernel,
        grid=(n // tm, nj),
        in_specs=[
            pl.BlockSpec((tm, d), lambda i, j: (i, 0)),
            pl.BlockSpec((tm, NA_WIDTH), lambda i, j: (i, 0)),
            pl.BlockSpec((tm, GLA_V), lambda i, j: (i, 0)),
            pl.BlockSpec((d, tn), lambda i, j: (0, j)),
            pl.BlockSpec((1, tn), lambda i, j: (0, j)),
            pl.BlockSpec((d, tn), lambda i, j: (0, nj + j)),
            pl.BlockSpec((1, tn), lambda i, j: (0, nj + j)),
            pl.BlockSpec((NA_WIDTH, tn), lambda i, j: (0, j)),
            pl.BlockSpec((GLA_V, tn), lambda i, j: (0, j)),
        ],
        out_specs=pl.BlockSpec((tm, tn), lambda i, j: (i, j)),
        out_shape=jax.ShapeDtypeStruct((n, d), bf16),
        compiler_params=_params(("parallel", "arbitrary")),
        name="merge",
    )(h0b, y_na, og, w_gate, b_gate, w_gate, b_gate, w_pa, w_pb)


def _out_ln_kernel(x_ref, ge_ref, be_ref, m_ref, w_ref, g_ref, b_ref, h1_ref, h1b_ref):
    h0 = _layer_norm(x_ref[...], ge_ref[...], be_ref[...])
    y = _dot(m_ref[...], w_ref[...])
    h1 = _layer_norm(DEEPNORM_ALPHA * h0 + y, g_ref[...], b_ref[...])
    h1_ref[...] = h1
    h1b_ref[...] = h1.astype(bf16)


def _out_ln(x, ge, be, merged, w_out, g, b, tm):
    n, d = x.shape
    row = pl.BlockSpec((tm, d), lambda i: (i, 0))
    vec = pl.BlockSpec((1, d), lambda i: (0, 0))
    return pl.pallas_call(
        _out_ln_kernel,
        grid=(n // tm,),
        in_specs=[row, vec, vec, row, pl.BlockSpec((d, d), lambda i: (0, 0)), vec, vec],
        out_specs=[row, row],
        out_shape=[jax.ShapeDtypeStruct((n, d), f32), jax.ShapeDtypeStruct((n, d), bf16)],
        compiler_params=_params(("parallel",)),
        name="out_ln",
    )(x, ge, be, merged, w_out, g, b)


def _peer_prep_kernel(h1b_ref, h1_ref, p_ref, wq_ref, sk_ref, wg_ref, wp_ref, sc_ref, res_ref):
    h = h1b_ref[...]
    q = _dot(h, wq_ref[...]).astype(bf16)
    half = PEER_DKEY // 2
    for j in range(sk_ref.shape[0]):
        sc_ref[j] = lax.dot_general(sk_ref[j], q[:, j * half:(j + 1) * half], _NT, preferred_element_type=f32)
    gate = jax.nn.sigmoid(_dot(h, wg_ref[...]))
    ple = _dot(p_ref[...].astype(bf16), wp_ref[...]) * gate
    res_ref[...] = DEEPNORM_ALPHA * h1_ref[...] + ple


def _peer_prep(h1b, h1, p, wq, sk, wg, wp, tm, tn):
    n, d = h1b.shape
    half = PEER_DKEY // 2
    jn = tn // half
    return pl.pallas_call(
        _peer_prep_kernel,
        grid=(n // tm, d // tn),
        in_specs=[
            pl.BlockSpec((tm, d), lambda i, j: (i, 0)),
            pl.BlockSpec((tm, tn), lambda i, j: (i, j)),
            pl.BlockSpec((tm, PLE_DIM), lambda i, j: (i, 0)),
            pl.BlockSpec((d, tn), lambda i, j: (0, j)),
            pl.BlockSpec((jn, PEER_NKEYS, half), lambda i, j: (j, 0, 0)),
            pl.BlockSpec((d, tn), lambda i, j: (0, j)),
            pl.BlockSpec((PLE_DIM, tn), lambda i, j: (0, j)),
        ],
        out_specs=[
            pl.BlockSpec((jn, PEER_NKEYS, tm), lambda i, j: (j, 0, i)),
            pl.BlockSpec((tm, tn), lambda i, j: (i, j)),
        ],
        out_shape=[
            jax.ShapeDtypeStruct((2 * PEER_HEADS, PEER_NKEYS, n), f32),
            jax.ShapeDtypeStruct((n, d), f32),
        ],
        compiler_params=_params(("parallel", "arbitrary")),
        name="peer_prep",
    )(h1b, h1, p, wq, sk, wg, wp)


_THR_ROWS = 8


def _top_rows(v, k_top, out_ref):
    n_rows = v.shape[0]
    iota = lax.broadcasted_iota(jnp.int32, v.shape, 0)

    def body(k, v):
        m = jnp.max(v, axis=0, keepdims=True)
        first = jnp.min(jnp.where(v == m, iota, n_rows), axis=0, keepdims=True)
        out_ref[pl.ds(k, 1), :] = m
        return jnp.where(iota == first, -jnp.inf, v)

    lax.fori_loop(0, k_top, body, v)


def _peer_topk_kernel(sc_ref, thr_ref, v1_ref, v2_ref, top_ref):
    kk = PEER_TOPK
    _top_rows(sc_ref[0], kk, v1_ref)
    _top_rows(sc_ref[1], kk, v2_ref)
    val1 = v1_ref[...]
    val2 = v2_ref[...]
    t = val1.shape[-1]
    cand = (val1[:, None, :] + val2[None, :, :]).reshape(kk * kk, t)
    _top_rows(cand, kk, top_ref)
    top = top_ref[...]
    m = top[0:1]
    zsum = jnp.sum(jnp.exp(top - m), axis=0, keepdims=True)
    thr_ref[0] = jnp.zeros((_THR_ROWS, t), f32)
    thr_ref[0, 0:1, :] = top[kk - 1:kk]
    thr_ref[0, 1:2, :] = val1[0:1]
    thr_ref[0, 2:3, :] = val2[0:1]
    thr_ref[0, 3:4, :] = zsum


def _peer_topk(sc, tt):
    _, nk, n = sc.shape
    return pl.pallas_call(
        _peer_topk_kernel,
        grid=(n // tt, PEER_HEADS),
        in_specs=[pl.BlockSpec((2, nk, tt), lambda i, h: (h, 0, i))],
        out_specs=pl.BlockSpec((1, _THR_ROWS, tt), lambda i, h: (h, 0, i)),
        out_shape=jax.ShapeDtypeStruct((PEER_HEADS, _THR_ROWS, n), f32),
        scratch_shapes=[pltpu.VMEM((PEER_TOPK, tt), f32)] * 3,
        compiler_params=_params(("parallel", "arbitrary")),
        name="peer_topk",
    )(sc)


def _peer_dense_kernel(h1b_ref, u_ref, vt_ref, sc_ref, thr_ref, res_ref, g_ref, b_ref, out_ref,
                       acc_ref, e1_ref, e2_ref):
    e = pl.program_id(1)
    nk = PEER_NKEYS

    @pl.when(e == 0)
    def _():
        acc_ref[...] = jnp.zeros_like(acc_ref)
        for h in range(PEER_HEADS):
            e1_ref[h] = jnp.exp(sc_ref[2 * h] - thr_ref[h, 1:2, :])
            e2_ref[h] = jnp.exp(sc_ref[2 * h + 1] - thr_ref[h, 2:3, :]) / thr_ref[h, 3:4, :]

    hid = lax.dot_general(u_ref[...], h1b_ref[...], _NT, preferred_element_type=f32)
    n_i1 = u_ref.shape[0] // nk
    ws = []
    for a in range(n_i1):
        i1 = e * n_i1 + a
        gate = jnp.zeros((nk, hid.shape[1]), f32)
        for h in range(PEER_HEADS):
            s = sc_ref[2 * h, pl.ds(i1, 1), :] + sc_ref[2 * h + 1]
            g = e1_ref[h, pl.ds(i1, 1), :] * e2_ref[h]
            gate = gate + jnp.where(s >= thr_ref[h, 0:1, :], g, 0.0)
        x = hid[a * nk:(a + 1) * nk]
        gelu = 0.5 * x * (1.0 + lax.erf(x * math.sqrt(0.5)))
        ws.append((gate * gelu).astype(bf16))
    acc_ref[...] += _dot(vt_ref[...], jnp.concatenate(ws, axis=0))

    @pl.when(e == pl.num_programs(1) - 1)
    def _():
        out_ref[...] = _layer_norm(res_ref[...] + acc_ref[...].T, g_ref[...], b_ref[...])


def _peer_dense(h1b, u_b, vt_b, sc, thr, res, g, b, tt, te):
    n, d = h1b.shape
    n_e = u_b.shape[0]
    vec = pl.BlockSpec((1, d), lambda i, e: (0, 0))
    return pl.pallas_call(
        _peer_dense_kernel,
        grid=(n // tt, n_e // te),
        in_specs=[
            pl.BlockSpec((tt, d), lambda i, e: (i, 0)),
            pl.BlockSpec((te, d), lambda i, e: (e, 0)),
            pl.BlockSpec((d, te), lambda i, e: (0, e)),
            pl.BlockSpec((2 * PEER_HEADS, PEER_NKEYS, tt), lambda i, e: (0, 0, i)),
            pl.BlockSpec((PEER_HEADS, _THR_ROWS, tt), lambda i, e: (0, 0, i)),
            pl.BlockSpec((tt, d), lambda i, e: (i, 0)),
            vec, vec,
        ],
        out_specs=pl.BlockSpec((tt, d), lambda i, e: (i, 0)),
        out_shape=jax.ShapeDtypeStruct((n, d), f32),
        scratch_shapes=[
            pltpu.VMEM((d, tt), f32),
            pltpu.VMEM((PEER_HEADS, PEER_NKEYS, tt), f32),
            pltpu.VMEM((PEER_HEADS, PEER_NKEYS, tt), f32),
        ],
        compiler_params=_params(("parallel", "arbitrary")),
        name="peer_dense",
    )(h1b, u_b, vt_b, sc, thr, res, g, b)


def _tile(n, pref):
    return min(n, pref)


def kernel(x, p, ln_emb_g, ln_emb_b, w_in, b_in, na_rpb, gla_gk_up_f, gla_gk_bias_f, gla_gk_up_b, gla_gk_bias_b, gla_norm_g, w_proj_a, w_proj_b, w_out, ln_a_g, ln_a_b, peer_wq, peer_sub_keys, peer_u, peer_v, w_ple, w_ple_gate, ln_b_g, ln_b_b):
    bsz, seq, d = x.shape
    n = bsz * seq
    assert w_in.shape[0] == 1, "single-layer encoder"
    xf = x.reshape(n, d)
    pf = p[0].reshape(n, PLE_DIM)
    vec = lambda a: a.reshape(1, -1)

    w_in0 = w_in[0]
    b_in0 = vec(b_in[0])
    w_lr = w_in0[:, _OFF_LR:_OFF_GATE]
    b_lr = b_in0[:, _OFF_LR:_OFF_GATE]
    w_gate = w_in0[:, _OFF_GATE:]
    b_gate = b_in0[:, _OFF_GATE:]

    z, h0b, lr = _ln_inproj(xf, vec(ln_emb_g), vec(ln_emb_b), w_in0, b_in0, w_lr, b_lr,
                            tm=_tile(n, 1024), tn=512)

    y_na = _natten(z, _na_bias_table(na_rpb[0], seq // GRID_W), bsz, seq)

    o_f = _gla(z, lr, gla_gk_up_f[0], vec(gla_gk_bias_f[0]), bsz, seq, reverse=False)
    og = _gla(z, lr, gla_gk_up_b[0], vec(gla_gk_bias_b[0]), bsz, seq, reverse=True,
              prev=o_f, gnorm=vec(gla_norm_g[0]))

    merged = _merge(h0b, y_na, og, w_gate, b_gate, w_proj_a[0].astype(bf16), w_proj_b[0].astype(bf16),
                    tm=_tile(n, 1024), tn=256)

    h1, h1b = _out_ln(xf, vec(ln_emb_g), vec(ln_emb_b), merged, w_out[0].astype(bf16),
                      vec(ln_a_g[0]), vec(ln_a_b[0]), tm=_tile(n, 512))

    sk = peer_sub_keys[0].reshape(2 * PEER_HEADS, PEER_NKEYS, PEER_DKEY // 2).astype(bf16)
    sc, res = _peer_prep(h1b, h1, pf, peer_wq[0].astype(bf16), sk, w_ple_gate[0].astype(bf16),
                         w_ple[0].astype(bf16), tm=_tile(n, 512), tn=512)

    thr = _peer_topk(sc, tt=_tile(n, 256))

    out = _peer_dense(h1b, peer_u[0].astype(bf16), peer_v[0].T.astype(bf16), sc, thr, res,
                      vec(ln_b_g[0]), vec(ln_b_b[0]), tt=_tile(n, 512), te=512)
    return out.reshape(bsz, seq, d)
```

```python
import functools
import math

import jax
import jax.numpy as jnp
from jax import lax
from jax.experimental import pallas as pl
from jax.experimental.pallas import tpu as pltpu

f32 = jnp.float32
bf16 = jnp.bfloat16

D_MODEL = 2048
GRID_W = 64
NA_HEADS = 16
NA_HEAD_DIM = 64
NA_WIDTH = NA_HEADS * NA_HEAD_DIM
NA_WIN_R = 8
NA_WIN_C = 16
GLA_HEADS = 4
GLA_DK = D_MODEL // (2 * GLA_HEADS)
GLA_DV = D_MODEL // GLA_HEADS
GLA_K = GLA_HEADS * GLA_DK
GLA_V = GLA_HEADS * GLA_DV
GLA_GATE_RANK = 16
GLA_TAU = 16.0
GLA_CHUNK = 64
PEER_HEADS = 8
PEER_NKEYS = 128
PEER_N = PEER_NKEYS * PEER_NKEYS
PEER_DKEY = 256
PEER_TOPK = 16
PLE_DIM = 256
LN_EPS = 1e-5
DEPTH = 1
DEEPNORM_ALPHA = (2.0 * DEPTH) ** 0.25

_OFF_NA_Q = 0
_OFF_GLA_Q = 3 * NA_WIDTH
_OFF_GLA_K = _OFF_GLA_Q + GLA_K
_OFF_GLA_V = _OFF_GLA_K + GLA_K
_OFF_GLA_G = _OFF_GLA_V + GLA_V
_OFF_LR = _OFF_GLA_G + GLA_V
_OFF_GATE = _OFF_LR + 2 * GLA_GATE_RANK
Z_MAIN = _OFF_LR

LANES = 128
VMEM_LIMIT = 56 * 1024 * 1024

_NT = (((1,), (1,)), ((), ()))
_TN = (((0,), (0,)), ((), ()))


def _params(sem, vmem=VMEM_LIMIT):
    return pltpu.CompilerParams(dimension_semantics=sem, vmem_limit_bytes=vmem)


def _layer_norm(x, g, b):
    mu = jnp.mean(x, axis=-1, keepdims=True)
    var = jnp.mean(jnp.square(x - mu), axis=-1, keepdims=True)
    return (x - mu) * lax.rsqrt(var + LN_EPS) * g + b


def _dot(a, b):
    return jnp.dot(a, b, preferred_element_type=f32)


def _ln_inproj_kernel(x_ref, g_ref, b_ref, w_ref, bias_ref, wlr_ref, blr_ref, z_ref, h0b_ref, lr_ref):
    @pl.when(pl.program_id(1) == 0)
    def _():
        hb = _layer_norm(x_ref[...], g_ref[...], b_ref[...]).astype(bf16)
        h0b_ref[...] = hb
        lr_ref[...] = _dot(hb, wlr_ref[...].astype(bf16)) + blr_ref[...]

    z = _dot(h0b_ref[...], w_ref[...].astype(bf16)) + bias_ref[...]
    z_ref[...] = z.astype(bf16)


def _ln_inproj(x, g, b, w_in, b_in, w_lr, b_lr, tm, tn):
    n, d = x.shape
    r = w_lr.shape[1]
    return pl.pallas_call(
        _ln_inproj_kernel,
        grid=(n // tm, Z_MAIN // tn),
        in_specs=[
            pl.BlockSpec((tm, d), lambda i, j: (i, 0)),
            pl.BlockSpec((1, d), lambda i, j: (0, 0)),
            pl.BlockSpec((1, d), lambda i, j: (0, 0)),
            pl.BlockSpec((d, tn), lambda i, j: (0, j)),
            pl.BlockSpec((1, tn), lambda i, j: (0, j)),
            pl.BlockSpec((d, r), lambda i, j: (0, 0)),
            pl.BlockSpec((1, r), lambda i, j: (0, 0)),
        ],
        out_specs=[
            pl.BlockSpec((tm, tn), lambda i, j: (i, j)),
            pl.BlockSpec((tm, d), lambda i, j: (i, 0)),
            pl.BlockSpec((tm, r), lambda i, j: (i, 0)),
        ],
        out_shape=[
            jax.ShapeDtypeStruct((n, Z_MAIN), bf16),
            jax.ShapeDtypeStruct((n, d), bf16),
            jax.ShapeDtypeStruct((n, r), f32),
        ],
        compiler_params=_params(("parallel", "arbitrary")),
        name="ln_inproj",
    )(x, g, b, w_in, b_in, w_lr, b_lr)


def _na_bias_table(rpb, rows):
    w = GRID_W
    kr = min(NA_WIN_R, rows)
    kc = NA_WIN_C
    nh, ndr, _ = rpb.shape
    ext = jnp.pad(rpb.astype(f32), ((0, 0), (0, 0), (w - kc, w - kc)))
    flat = jnp.broadcast_to(ext[:, :, None, :], (nh, ndr, w, 2 * w - 1)).reshape(nh, ndr, w * (2 * w - 1))
    skew = flat[:, :, w - 1:w - 1 + w * (2 * w - 2)].reshape(nh, ndr, w, 2 * w - 2)[..., :w]
    cols = jnp.arange(w)
    c0 = jnp.clip(cols - kc // 2, 0, w - kc)
    valid = (cols[None, :] >= c0[:, None]) & (cols[None, :] < c0[:, None] + kc)
    skew = jnp.where(valid[None, None], skew, -1e30)
    tabs = []
    for d in range(kr):
        lo = NA_WIN_R - 1 - d
        tabs.append(skew[:, lo:lo + kr].transpose(0, 2, 1, 3).reshape(nh, w, kr * w))
    return jnp.stack(tabs)


def _natten_kernel(q_ref, k_ref, v_ref, bias_ref, o_ref, *, rows, kr):
    r = pl.program_id(1)
    r0 = jnp.clip(r - kr // 2, 0, rows - kr)
    start = pl.multiple_of(r0 * GRID_W, GRID_W)
    win = kr * GRID_W
    lane = lax.broadcasted_iota(jnp.int32, (GRID_W, LANES), 1)
    lo = lane < NA_HEAD_DIM
    heads_per_group = LANES // NA_HEAD_DIM
    for grp in range(NA_HEADS // heads_per_group):
        cs = slice(grp * LANES, (grp + 1) * LANES)
        qp = q_ref[:, cs]
        kp = k_ref[pl.ds(start, win), cs]
        vp = v_ref[pl.ds(start, win), cs]
        outs = []
        for half in range(heads_per_group):
            sel = lo if half == 0 else jnp.logical_not(lo)
            qm = jnp.where(sel, qp, jnp.zeros_like(qp))
            s = lax.dot_general(qm, kp, _NT, preferred_element_type=f32) * (NA_HEAD_DIM ** -0.5)
            s = s + bias_ref[0, grp * heads_per_group + half]
            p = jnp.exp(s - jnp.max(s, axis=-1, keepdims=True))
            l = jnp.sum(p, axis=-1, keepdims=True)
            outs.append(_dot(p.astype(bf16), vp) / l)
        o_ref[:, cs] = jnp.where(lo, outs[0], outs[1]).astype(bf16)


def _natten(z, bias_tab, bsz, seq):
    n = z.shape[0]
    rows = seq // GRID_W
    kr = min(NA_WIN_R, rows)

    def d_of(r):
        return r - jnp.clip(r - kr // 2, 0, rows - kr)

    return pl.pallas_call(
        functools.partial(_natten_kernel, rows=rows, kr=kr),
        grid=(bsz, rows),
        in_specs=[
            pl.BlockSpec((GRID_W, NA_WIDTH), lambda b, r: (b * rows + r, 0)),
            pl.BlockSpec((seq, NA_WIDTH), lambda b, r: (b, 1)),
            pl.BlockSpec((seq, NA_WIDTH), lambda b, r: (b, 2)),
            pl.BlockSpec((1, NA_HEADS, GRID_W, kr * GRID_W), lambda b, r: (d_of(r), 0, 0, 0)),
        ],
        out_specs=pl.BlockSpec((GRID_W, NA_WIDTH), lambda b, r: (b * rows + r, 0)),
        out_shape=jax.ShapeDtypeStruct((n, NA_WIDTH), bf16),
        compiler_params=_params(("parallel", "arbitrary")),
        name="natten",
    )(z, z, z, bias_tab)


def _gla_kernel(*refs, reverse):
    if reverse:
        (q_ref, k_ref, v0_ref, v1_ref, lr_ref, up_ref, gb_ref, prev_ref, g0_ref, g1_ref, gn_ref,
         o_ref, state_ref) = refs
    else:
        q_ref, k_ref, v0_ref, v1_ref, lr_ref, up_ref, gb_ref, o_ref, state_ref = refs

    @pl.when(pl.program_id(1) == 0)
    def _():
        state_ref[...] = jnp.zeros_like(state_ref)

    c = GLA_CHUNK
    row = lax.broadcasted_iota(jnp.int32, (c, c), 0)
    col = lax.broadcasted_iota(jnp.int32, (c, c), 1)
    if reverse:
        tri = (col >= row).astype(f32)
        mask = col > row
        last = 0
        lr = lr_ref[:, GLA_GATE_RANK:2 * GLA_GATE_RANK]
    else:
        tri = (col <= row).astype(f32)
        mask = col <= row
        last = c - 1
        lr = lr_ref[:, 0:GLA_GATE_RANK]
    lr = lr.astype(bf16)
    v_refs = (v0_ref, v1_ref)
    heads_per_ref = GLA_HEADS // 2
    for h in range(GLA_HEADS):
        ks = slice(h * GLA_DK, (h + 1) * GLA_DK)
        vs = slice((h % heads_per_ref) * GLA_DV, (h % heads_per_ref + 1) * GLA_DV)
        la = jax.nn.log_sigmoid(_dot(lr, up_ref[:, ks].astype(bf16)) + gb_ref[:, ks]) * (1.0 / GLA_TAU)
        b = jnp.dot(tri, la, preferred_element_type=f32, precision=lax.Precision.HIGHEST)
        b_last = b[last:last + 1, :]
        q = q_ref[:, ks].astype(f32) * (GLA_DK ** -0.5)
        k = k_ref[:, ks].astype(f32)
        v = v_refs[h // heads_per_ref][:, vs]
        qd = (q * jnp.exp(b)).astype(bf16)
        ki = (k * jnp.exp(-b)).astype(bf16)
        kst = (k * jnp.exp(b_last - b)).astype(bf16)
        attn = lax.dot_general(qd, ki, _NT, preferred_element_type=f32)
        attn = jnp.where(mask, attn, 0.0)
        st = state_ref[h]
        o = _dot(attn.astype(bf16), v) + lax.dot_general(qd, st.astype(bf16), _NT, preferred_element_type=f32)
        state_ref[h] = st * jnp.exp(b_last) + lax.dot_general(v, kst, _TN, preferred_element_type=f32)
        os_ = slice(h * GLA_DV, (h + 1) * GLA_DV)
        if reverse:
            o = o + prev_ref[:, os_]
            o = o * lax.rsqrt(jnp.mean(o * o, axis=-1, keepdims=True) + LN_EPS) * gn_ref[...]
            gate = (g0_ref, g1_ref)[h // heads_per_ref][:, vs].astype(f32)
            o_ref[:, os_] = (o * (gate * jax.nn.sigmoid(gate))).astype(bf16)
        else:
            o_ref[:, os_] = o


def _gla(z, lr, up, gbias, bsz, seq, reverse, prev=None, gnorm=None):
    n = z.shape[0]
    c = GLA_CHUNK
    nc = seq // c
    half_v = GLA_V // 2

    if reverse:
        def blk(b, i):
            return b * nc + (nc - 1 - i)
    else:
        def blk(b, i):
            return b * nc + i

    def zspec(width, off):
        assert off % width == 0
        return pl.BlockSpec((c, width), lambda b, i: (blk(b, i), off // width))

    in_specs = [
        zspec(GLA_K, _OFF_GLA_Q), zspec(GLA_K, _OFF_GLA_K),
        zspec(half_v, _OFF_GLA_V), zspec(half_v, _OFF_GLA_V + half_v),
        pl.BlockSpec((c, 2 * GLA_GATE_RANK), lambda b, i: (blk(b, i), 0)),
        pl.BlockSpec((GLA_GATE_RANK, GLA_K), lambda b, i: (0, 0)),
        pl.BlockSpec((1, GLA_K), lambda b, i: (0, 0)),
    ]
    args = [z, z, z, z, lr, up, gbias]
    if reverse:
        in_specs += [
            pl.BlockSpec((c, GLA_V), lambda b, i: (blk(b, i), 0)),
            zspec(half_v, _OFF_GLA_G), zspec(half_v, _OFF_GLA_G + half_v),
            pl.BlockSpec((1, GLA_DV), lambda b, i: (0, 0)),
        ]
        args += [prev, z, z, gnorm]
    return pl.pallas_call(
        functools.partial(_gla_kernel, reverse=reverse),
        grid=(bsz, nc),
        in_specs=in_specs,
        out_specs=pl.BlockSpec((c, GLA_V), lambda b, i: (blk(b, i), 0)),
        out_shape=jax.ShapeDtypeStruct((n, GLA_V), bf16 if reverse else f32),
        scratch_shapes=[pltpu.VMEM((GLA_HEADS, GLA_DV, GLA_DK), f32)],
        compiler_params=_params(("parallel", "arbitrary")),
        name="gla_bwd" if reverse else "gla_fwd",
    )(*args)


def _merge_kernel(h_ref, ya_ref, yb_ref, wga_ref, bga_ref, wgb_ref, bgb_ref, wpa_ref, wpb_ref, o_ref):
    h = h_ref[...]
    ga = jax.nn.sigmoid(_dot(h, wga_ref[...].astype(bf16)) + bga_ref[...])
    gb = jax.nn.sigmoid(_dot(h, wgb_ref[...].astype(bf16)) + bgb_ref[...])
    ya = _dot(ya_ref[...], wpa_ref[...])
    yb = _dot(yb_ref[...], wpb_ref[...])
    o_ref[...] = (ga * ya + gb * yb).astype(bf16)


def _merge(h0b, y_na, og, w_gate, b_gate, w_pa, w_pb, tm, tn):
    n, d = h0b.shape
    nj = d // tn
    return pl.pallas_call(
        _merge_kernel,
        grid=(n // tm, nj),
        in_specs=[
            pl.BlockSpec((tm, d), lambda i, j: (i, 0)),
            pl.BlockSpec((tm, NA_WIDTH), lambda i, j: (i, 0)),
            pl.BlockSpec((tm, GLA_V), lambda i, j: (i, 0)),
            pl.BlockSpec((d, tn), lambda i, j: (0, j)),
            pl.BlockSpec((1, tn), lambda i, j: (0, j)),
            pl.BlockSpec((d, tn), lambda i, j: (0, nj + j)),
            pl.BlockSpec((1, tn), lambda i, j: (0, nj + j)),
            pl.BlockSpec((NA_WIDTH, tn), lambda i, j: (0, j)),
            pl.BlockSpec((GLA_V, tn), lambda i, j: (0, j)),
        ],
        out_specs=pl.BlockSpec((tm, tn), lambda i, j: (i, j)),
        out_shape=jax.ShapeDtypeStruct((n, d), bf16),
        compiler_params=_params(("parallel", "arbitrary")),
        name="merge",
    )(h0b, y_na, og, w_gate, b_gate, w_gate, b_gate, w_pa, w_pb)


def _out_ln_kernel(x_ref, ge_ref, be_ref, m_ref, w_ref, g_ref, b_ref, h1_ref, h1b_ref):
    h0 = _layer_norm(x_ref[...], ge_ref[...], be_ref[...])
    y = _dot(m_ref[...], w_ref[...])
    h1 = _layer_norm(DEEPNORM_ALPHA * h0 + y, g_ref[...], b_ref[...])
    h1_ref[...] = h1
    h1b_ref[...] = h1.astype(bf16)


def _out_ln(x, ge, be, merged, w_out, g, b, tm):
    n, d = x.shape
    row = pl.BlockSpec((tm, d), lambda i: (i, 0))
    vec = pl.BlockSpec((1, d), lambda i: (0, 0))
    return pl.pallas_call(
        _out_ln_kernel,
        grid=(n // tm,),
        in_specs=[row, vec, vec, row, pl.BlockSpec((d, d), lambda i: (0, 0)), vec, vec],
        out_specs=[row, row],
        out_shape=[jax.ShapeDtypeStruct((n, d), f32), jax.ShapeDtypeStruct((n, d), bf16)],
        compiler_params=_params(("parallel",)),
        name="out_ln",
    )(x, ge, be, merged, w_out, g, b)


def _peer_prep_kernel(h1b_ref, h1_ref, p_ref, wq_ref, sk_ref, wg_ref, wp_ref, sc_ref, res_ref):
    h = h1b_ref[...]
    q = _dot(h, wq_ref[...]).astype(bf16)
    half = PEER_DKEY // 2
    for j in range(sk_ref.shape[0]):
        sc_ref[j] = lax.dot_general(sk_ref[j], q[:, j * half:(j + 1) * half], _NT, preferred_element_type=f32)
    gate = jax.nn.sigmoid(_dot(h, wg_ref[...]))
    ple = _dot(p_ref[...].astype(bf16), wp_ref[...]) * gate
    res_ref[...] = DEEPNORM_ALPHA * h1_ref[...] + ple


def _peer_prep(h1b, h1, p, wq, sk, wg, wp, tm, tn):
    n, d = h1b.shape
    half = PEER_DKEY // 2
    jn = tn // half
    return pl.pallas_call(
        _peer_prep_kernel,
        grid=(n // tm, d // tn),
        in_specs=[
            pl.BlockSpec((tm, d), lambda i, j: (i, 0)),
            pl.BlockSpec((tm, tn), lambda i, j: (i, j)),
            pl.BlockSpec((tm, PLE_DIM), lambda i, j: (i, 0)),
            pl.BlockSpec((d, tn), lambda i, j: (0, j)),
            pl.BlockSpec((jn, PEER_NKEYS, half), lambda i, j: (j, 0, 0)),
            pl.BlockSpec((d, tn), lambda i, j: (0, j)),
            pl.BlockSpec((PLE_DIM, tn), lambda i, j: (0, j)),
        ],
        out_specs=[
            pl.BlockSpec((jn, PEER_NKEYS, tm), lambda i, j: (j, 0, i)),
            pl.BlockSpec((tm, tn), lambda i, j: (i, j)),
        ],
        out_shape=[
            jax.ShapeDtypeStruct((2 * PEER_HEADS, PEER_NKEYS, n), f32),
            jax.ShapeDtypeStruct((n, d), f32),
        ],
        compiler_params=_params(("parallel", "arbitrary")),
        name="peer_prep",
    )(h1b, h1, p, wq, sk, wg, wp)


_THR_ROWS = 8


SUBLANES = 8


def _candidate_pairs(kk):
    order = [0, 1, 2, 4, 3, 5, 6] + list(range(7, kk))
    return [(k1, k2) for k1 in order for k2 in range(kk // (k1 + 1))]


def _candidate_sums(val1, val2):
    kk, t = val1.shape
    pairs = _candidate_pairs(kk)
    n_pad = -len(pairs) % SUBLANES
    pairs = pairs + [None] * n_pad
    row = lax.broadcasted_iota(jnp.int32, (SUBLANES, t), 0)
    groups = []
    for g0 in range(0, len(pairs), SUBLANES):
        grp = pairs[g0:g0 + SUBLANES]
        a = jnp.full((SUBLANES, t), -jnp.inf, f32)
        b = jnp.zeros((SUBLANES, t), f32)
        off = 0
        while off < SUBLANES and grp[off] is not None:
            k1, k2 = grp[off]
            n = 1
            while (off + n < SUBLANES and grp[off + n] == (k1, k2 + n)
                   and (k2 + n) // SUBLANES == k2 // SUBLANES):
                n += 1
            blk = val2[(k2 // SUBLANES) * SUBLANES:(k2 // SUBLANES + 1) * SUBLANES]
            shift = (off - k2 % SUBLANES) % SUBLANES
            if shift:
                blk = pltpu.roll(blk, shift, 0)
            seg = (row >= off) & (row < off + n)
            a = jnp.where(seg, val1[k1:k1 + 1], a)
            b = jnp.where(seg, blk, b)
            off += n
        groups.append(a + b)
    return jnp.concatenate(groups, axis=0), n_pad


def _extract_top(vs, k_top, out_refs, exact):
    iotas = [lax.broadcasted_iota(jnp.int32, v.shape, 0) for v in vs]

    def body(k, vs):
        nxt = []
        for v, iota, out_ref in zip(vs, iotas, out_refs):
            m = jnp.max(v, axis=0, keepdims=True)
            out_ref[pl.ds(k, 1), :] = m
            hit = v == m
            if exact:
                first = jnp.min(jnp.where(hit, iota, v.shape[0]), axis=0, keepdims=True)
                hit = iota == first
            nxt.append(jnp.where(hit, -jnp.inf, v))
        return tuple(nxt)

    vs = lax.fori_loop(0, k_top, body, tuple(vs))
    return sum(jnp.sum((v == -jnp.inf).astype(f32), axis=0, keepdims=True) for v in vs)


def _peer_head_stats(sc_ref, thr_ref, v1_ref, v2_ref, top_ref, h, exact):
    kk = PEER_TOPK
    removed = _extract_top([sc_ref[2 * h], sc_ref[2 * h + 1]], kk, [v1_ref, v2_ref], exact)
    val1 = v1_ref[...]
    val2 = v2_ref[...]
    cand, n_pad = _candidate_sums(val1, val2)
    removed = removed + _extract_top([cand], kk, [top_ref], exact)
    top = top_ref[...]
    zsum = jnp.sum(jnp.exp(top - top[0:1]), axis=0, keepdims=True)
    thr_ref[h] = jnp.concatenate([
        top[kk - 1:kk],
        val1[0:1],
        val2[0:1],
        zsum,
        jnp.zeros((_THR_ROWS - 4, top.shape[1]), f32),
    ], axis=0)
    return removed - (3 * kk + n_pad)


def _peer_topk_kernel(sc_ref, thr_ref, v1_ref, v2_ref, top_ref):
    def fast(h, extra):
        return jnp.maximum(extra, _peer_head_stats(sc_ref, thr_ref, v1_ref, v2_ref, top_ref, h, exact=False))

    extra = lax.fori_loop(0, PEER_HEADS, fast, jnp.zeros((1, sc_ref.shape[-1]), f32))

    @pl.when(jnp.max(extra) > 0.0)
    def _():
        def slow(h, carry):
            _peer_head_stats(sc_ref, thr_ref, v1_ref, v2_ref, top_ref, h, exact=True)
            return carry

        lax.fori_loop(0, PEER_HEADS, slow, 0)


def _peer_topk(sc, tt):
    n_half, nk, n = sc.shape
    return pl.pallas_call(
        _peer_topk_kernel,
        grid=(n // tt,),
        in_specs=[pl.BlockSpec((n_half, nk, tt), lambda i: (0, 0, i))],
        out_specs=pl.BlockSpec((PEER_HEADS, _THR_ROWS, tt), lambda i: (0, 0, i)),
        out_shape=jax.ShapeDtypeStruct((PEER_HEADS, _THR_ROWS, n), f32),
        scratch_shapes=[pltpu.VMEM((PEER_TOPK, tt), f32)] * 3,
        compiler_params=_params(("parallel",)),
        name="peer_topk",
    )(sc)


def _peer_dense_kernel(h1b_ref, u_ref, vt_ref, sc_ref, thr_ref, res_ref, g_ref, b_ref, out_ref,
                       acc_ref, e1_ref, e2_ref, hid_ref, w_ref):
    e = pl.program_id(1)
    nk = PEER_NKEYS

    @pl.when(e == 0)
    def _():
        acc_ref[...] = jnp.zeros_like(acc_ref)
        for h in range(PEER_HEADS):
            e1_ref[h] = jnp.exp(sc_ref[2 * h] - thr_ref[h, 1:2, :])
            e2_ref[h] = jnp.exp(sc_ref[2 * h + 1] - thr_ref[h, 2:3, :]) / thr_ref[h, 3:4, :]

    hid_ref[...] = lax.dot_general(u_ref[...], h1b_ref[...], _NT, preferred_element_type=f32)
    n_i1 = u_ref.shape[0] // nk
    for a in range(n_i1):
        i1 = e * n_i1 + a
        rows = slice(a * nk, (a + 1) * nk)
        s1_rows = [sc_ref[2 * h, pl.ds(i1, 1), :] for h in range(PEER_HEADS)]
        e1_rows = [e1_ref[h, pl.ds(i1, 1), :] for h in range(PEER_HEADS)]
        for tc in range(hid_ref.shape[1] // LANES):
            ln = slice(tc * LANES, (tc + 1) * LANES)
            gate = jnp.zeros((nk, LANES), f32)
            for h in range(PEER_HEADS):
                s = s1_rows[h][:, ln] + sc_ref[2 * h + 1, :, ln]
                g = e1_rows[h][:, ln] * e2_ref[h, :, ln]
                gate = gate + jnp.where(s >= thr_ref[h, 0:1, ln], g, 0.0)
            x = hid_ref[rows, ln]
            gelu = 0.5 * x * (1.0 + lax.erf(x * math.sqrt(0.5)))
            w_ref[rows, ln] = (gate * gelu).astype(bf16)
    acc_ref[...] += _dot(vt_ref[...], w_ref[...])

    @pl.when(e == pl.num_programs(1) - 1)
    def _():
        out_ref[...] = _layer_norm(res_ref[...] + acc_ref[...].T, g_ref[...], b_ref[...])


def _peer_dense(h1b, u_b, vt_b, sc, thr, res, g, b, tt, te):
    n, d = h1b.shape
    n_e = u_b.shape[0]
    vec = pl.BlockSpec((1, d), lambda i, e: (0, 0))
    return pl.pallas_call(
        _peer_dense_kernel,
        grid=(n // tt, n_e // te),
        in_specs=[
            pl.BlockSpec((tt, d), lambda i, e: (i, 0)),
            pl.BlockSpec((te, d), lambda i, e: (e, 0)),
            pl.BlockSpec((d, te), lambda i, e: (0, e)),
            pl.BlockSpec((2 * PEER_HEADS, PEER_NKEYS, tt), lambda i, e: (0, 0, i)),
            pl.BlockSpec((PEER_HEADS, _THR_ROWS, tt), lambda i, e: (0, 0, i)),
            pl.BlockSpec((tt, d), lambda i, e: (i, 0)),
            vec, vec,
        ],
        out_specs=pl.BlockSpec((tt, d), lambda i, e: (i, 0)),
        out_shape=jax.ShapeDtypeStruct((n, d), f32),
        scratch_shapes=[
            pltpu.VMEM((d, tt), f32),
            pltpu.VMEM((PEER_HEADS, PEER_NKEYS, tt), f32),
            pltpu.VMEM((PEER_HEADS, PEER_NKEYS, tt), f32),
            pltpu.VMEM((te, tt), f32),
            pltpu.VMEM((te, tt), bf16),
        ],
        compiler_params=_params(("parallel", "arbitrary")),
        name="peer_dense",
    )(h1b, u_b, vt_b, sc, thr, res, g, b)


def _tile(n, pref):
    return min(n, pref)


def kernel(x, p, ln_emb_g, ln_emb_b, w_in, b_in, na_rpb, gla_gk_up_f, gla_gk_bias_f, gla_gk_up_b, gla_gk_bias_b, gla_norm_g, w_proj_a, w_proj_b, w_out, ln_a_g, ln_a_b, peer_wq, peer_sub_keys, peer_u, peer_v, w_ple, w_ple_gate, ln_b_g, ln_b_b):
    bsz, seq, d = x.shape
    n = bsz * seq
    assert w_in.shape[0] == 1, "single-layer encoder"
    xf = x.reshape(n, d)
    pf = p[0].reshape(n, PLE_DIM)
    vec = lambda a: a.reshape(1, -1)

    w_in0 = w_in[0]
    b_in0 = vec(b_in[0])
    w_lr = w_in0[:, _OFF_LR:_OFF_GATE]
    b_lr = b_in0[:, _OFF_LR:_OFF_GATE]
    w_gate = w_in0[:, _OFF_GATE:]
    b_gate = b_in0[:, _OFF_GATE:]

    z, h0b, lr = _ln_inproj(xf, vec(ln_emb_g), vec(ln_emb_b), w_in0, b_in0, w_lr, b_lr,
                            tm=_tile(n, 1024), tn=512)

    y_na = _natten(z, _na_bias_table(na_rpb[0], seq // GRID_W), bsz, seq)

    o_f = _gla(z, lr, gla_gk_up_f[0], vec(gla_gk_bias_f[0]), bsz, seq, reverse=False)
    og = _gla(z, lr, gla_gk_up_b[0], vec(gla_gk_bias_b[0]), bsz, seq, reverse=True,
              prev=o_f, gnorm=vec(gla_norm_g[0]))

    merged = _merge(h0b, y_na, og, w_gate, b_gate, w_proj_a[0].astype(bf16), w_proj_b[0].astype(bf16),
                    tm=_tile(n, 1024), tn=256)

    h1, h1b = _out_ln(xf, vec(ln_emb_g), vec(ln_emb_b), merged, w_out[0].astype(bf16),
                      vec(ln_a_g[0]), vec(ln_a_b[0]), tm=_tile(n, 512))

    sk = peer_sub_keys[0].reshape(2 * PEER_HEADS, PEER_NKEYS, PEER_DKEY // 2).astype(bf16)
    sc, res = _peer_prep(h1b, h1, pf, peer_wq[0].astype(bf16), sk, w_ple_gate[0].astype(bf16),
                         w_ple[0].astype(bf16), tm=_tile(n, 512), tn=512)

    thr = _peer_topk(sc, tt=_tile(n, LANES))

    out = _peer_dense(h1b, peer_u[0].astype(bf16), peer_v[0].T.astype(bf16), sc, thr, res,
                      vec(ln_b_g[0]), vec(ln_b_b[0]), tt=_tile(n, 512), te=512)
    return out.reshape(bsz, seq, d)
```
